```python
import jax, jax.numpy as jnp
from jax import lax
import numpy as np

D_MODEL = 1024
BATCH = 2
SEQ = 16384
DEPTH = 2

HEAD_DIM = 64
HEADS_PER_GROUP = 4
ATTN_PAIRS = ((128, 1), (512, 4), (2048, 16))
N_GROUPS = 3
N_ATTN_HEADS = N_GROUPS * HEADS_PER_GROUP
ATTN_WIDTH = N_ATTN_HEADS * HEAD_DIM
ATTN_OUT_WIDTH = HEADS_PER_GROUP * HEAD_DIM
ATTN_BLOCK = 128
ALIBI_MAX_EXP = 8.0
CONF_WIDTH = 768
CONF_KERNEL = 31
SC_WIDTH = 768
SC_KERNEL = 3
N_BRANCHES = 3
D_FF = 2816
FFN_KERNEL = 3
NORM_EPS = 1e-6
IN_WIDTHS = (ATTN_WIDTH, ATTN_WIDTH, ATTN_WIDTH, CONF_WIDTH, CONF_WIDTH, SC_WIDTH, SC_WIDTH, SC_WIDTH, N_BRANCHES * D_MODEL)
IN_WIDTH = 3 * ATTN_WIDTH + 2 * CONF_WIDTH + 3 * SC_WIDTH + N_BRANCHES * D_MODEL

kernel_name = 'hybrid_dilated_attn_conformer_shortconv_block'


def rms_norm(x, g):
    xf = x.astype(jnp.float32)
    y = xf * lax.rsqrt(jnp.mean(xf * xf, axis=-1, keepdims=True) + NORM_EPS)
    return (y * g.astype(jnp.float32)).astype(x.dtype)


def layer_norm(x, g, b):
    xf = x.astype(jnp.float32)
    mu = jnp.mean(xf, axis=-1, keepdims=True)
    xc = xf - mu
    y = xc * lax.rsqrt(jnp.mean(xc * xc, axis=-1, keepdims=True) + NORM_EPS)
    return (y * g.astype(jnp.float32) + b.astype(jnp.float32)).astype(x.dtype)


def causal_dwconv(x, w):
    K, C = w.shape
    return lax.conv_general_dilated(
        x, w[:, None, :].astype(x.dtype), window_strides=(1,), padding=[(K - 1, 0)],
        dimension_numbers=('NWC', 'WIO', 'NWC'), feature_group_count=C)


def dilated_window_attention(q, k, v, window, dilation, slopes):
    B, S, H, hd = q.shape
    L = S // dilation
    span = window // dilation
    nb = -(-L // ATTN_BLOCK)
    Lp = nb * ATTN_BLOCK

    def to_sub(t):
        return t.reshape(B, L, dilation, H, hd).transpose(0, 2, 3, 1, 4)

    qb = jnp.pad(to_sub(q), ((0, 0), (0, 0), (0, 0), (0, Lp - L), (0, 0)))
    qb = qb.reshape(B, dilation, H, nb, ATTN_BLOCK, hd)

    def key_windows(t):
        tp = jnp.pad(to_sub(t), ((0, 0), (0, 0), (0, 0), (ATTN_BLOCK, Lp - L), (0, 0)))
        tp = tp.reshape(B, dilation, H, nb + 1, ATTN_BLOCK, hd)
        return jnp.concatenate([tp[:, :, :, :-1], tp[:, :, :, 1:]], axis=4)

    kw = key_windows(k)
    vw = key_windows(v)
    s = jnp.einsum('brhnqd,brhnkd->brhnqk', qb, kw, preferred_element_type=jnp.float32)
    s = s * (hd ** -0.5)
    qi = jnp.arange(ATTN_BLOCK)[:, None]
    ki = jnp.arange(2 * ATTN_BLOCK)[None, :]
    steps = qi + ATTN_BLOCK - ki
    blk = jnp.arange(nb)[:, None, None]
    valid = (steps >= 0) & (steps <= span) & (blk * ATTN_BLOCK + ki - ATTN_BLOCK >= 0)
    dist = (steps * dilation).astype(jnp.float32)
    s = s - slopes.astype(jnp.float32)[None, None, :, None, None, None] * dist
    s = jnp.where(valid, s, -jnp.inf)
    m = jnp.max(s, axis=-1, keepdims=True)
    p = jnp.exp(s - m)
    l = jnp.sum(p, axis=-1, keepdims=True)
    o = jnp.einsum('brhnqk,brhnkd->brhnqd', p, vw.astype(jnp.float32)) / l
    lse = (m + jnp.log(l))[..., 0]

    def from_sub(t):
        t = t.reshape((B, dilation, H, Lp) + t.shape[5:])[:, :, :, :L]
        t = jnp.moveaxis(t, 3, 1)
        return t.reshape((B, S, H) + t.shape[4:])

    return from_sub(o), from_sub(lse)


def hybrid_layer(x, norm1_g, w_in, conf_dw_w, conf_dw_b, conf_ln_g, conf_ln_b, w_conf_out,
                 sc_dw_w, w_sc_out, w_attn_out, w_o, norm2_g, w_up, ffn_dw_w, w_down):
    B, S, D = x.shape
    h = rms_norm(x, norm1_g)
    u = h @ w_in
    split_points = [int(c) for c in np.cumsum(IN_WIDTHS)[:-1]]
    q, k, v, conf_a, conf_gate, sc_b, sc_c, sc_x, gates = jnp.split(u, split_points, axis=-1)

    q = q.reshape(B, S, N_GROUPS, HEADS_PER_GROUP, HEAD_DIM)
    k = k.reshape(B, S, N_GROUPS, HEADS_PER_GROUP, HEAD_DIM)
    v = v.reshape(B, S, N_GROUPS, HEADS_PER_GROUP, HEAD_DIM)
    slopes = jnp.exp2(-ALIBI_MAX_EXP * jnp.arange(1, N_ATTN_HEADS + 1, dtype=jnp.float32) / N_ATTN_HEADS)
    slopes = slopes.reshape(N_GROUPS, HEADS_PER_GROUP)
    outs, lses = [], []
    for gi, (window, dilation) in enumerate(ATTN_PAIRS):
        o_g, lse_g = dilated_window_attention(q[:, :, gi], k[:, :, gi], v[:, :, gi], window, dilation, slopes[gi])
        outs.append(o_g)
        lses.append(lse_g)
    wts = jax.nn.softmax(jnp.stack(lses, axis=0), axis=0)
    o = jnp.sum(wts[..., None] * jnp.stack(outs, axis=0), axis=0)
    attn = o.reshape(B, S, ATTN_OUT_WIDTH).astype(x.dtype) @ w_attn_out

    a = conf_a * jax.nn.sigmoid(conf_gate)
    a = causal_dwconv(a, conf_dw_w) + conf_dw_b
    a = jax.nn.silu(layer_norm(a, conf_ln_g, conf_ln_b))
    conf = a @ w_conf_out

    short = (sc_b * causal_dwconv(sc_c * sc_x, sc_dw_w)) @ w_sc_out

    g = jax.nn.sigmoid(gates.reshape(B, S, N_BRANCHES, D))
    mixed = g[:, :, 0] * attn + g[:, :, 1] * conf + g[:, :, 2] * short
    x = x + mixed @ w_o

    h = rms_norm(x, norm2_g)
    up = causal_dwconv(h @ w_up, ffn_dw_w)
    f_gate, f_val = jnp.split(up, 2, axis=-1)
    return x + (jax.nn.silu(f_gate) * f_val) @ w_down


def setup_inputs(seed: int = 0) -> dict:
    key = jax.random.key(seed)
    ks = jax.random.split(key, 20)
    f32 = jnp.float32

    def dense(k_, shape, fan_in):
        return jax.random.normal(k_, shape, f32) * (fan_in ** -0.5)

    def gain(k_, shape):
        return 1.0 + 0.01 * jax.random.normal(k_, shape, f32)

    return {
        'x': jax.random.normal(ks[0], (BATCH, SEQ, D_MODEL), f32),
        'norm1_g': gain(ks[1], (DEPTH, D_MODEL)),
        'w_in': dense(ks[2], (DEPTH, D_MODEL, IN_WIDTH), D_MODEL),
        'conf_dw_w': dense(ks[3], (DEPTH, CONF_KERNEL, CONF_WIDTH), CONF_KERNEL),
        'conf_dw_b': 0.01 * jax.random.normal(ks[4], (DEPTH, CONF_WIDTH), f32),
        'conf_ln_g': gain(ks[5], (DEPTH, CONF_WIDTH)),
        'conf_ln_b': 0.01 * jax.random.normal(ks[6], (DEPTH, CONF_WIDTH), f32),
        'w_conf_out': dense(ks[7], (DEPTH, CONF_WIDTH, D_MODEL), CONF_WIDTH),
        'sc_dw_w': dense(ks[8], (DEPTH, SC_KERNEL, SC_WIDTH), SC_KERNEL),
        'w_sc_out': dense(ks[9], (DEPTH, SC_WIDTH, D_MODEL), SC_WIDTH),
        'w_attn_out': dense(ks[10], (DEPTH, ATTN_OUT_WIDTH, D_MODEL), ATTN_OUT_WIDTH),
        'w_o': dense(ks[11], (DEPTH, D_MODEL, D_MODEL), D_MODEL),
        'norm2_g': gain(ks[12], (DEPTH, D_MODEL)),
        'w_up': dense(ks[13], (DEPTH, D_MODEL, 2 * D_FF), D_MODEL),
        'ffn_dw_w': dense(ks[14], (DEPTH, FFN_KERNEL, 2 * D_FF), FFN_KERNEL),
        'w_down': dense(ks[15], (DEPTH, D_FF, D_MODEL), D_FF),
        'final_g': gain(ks[16], (D_MODEL,)),
    }


def reference(x, norm1_g, w_in, conf_dw_w, conf_dw_b, conf_ln_g, conf_ln_b, w_conf_out,
              sc_dw_w, w_sc_out, w_attn_out, w_o, norm2_g, w_up, ffn_dw_w, w_down, final_g):
    for layer in range(DEPTH):
        x = hybrid_layer(x, norm1_g[layer], w_in[layer], conf_dw_w[layer], conf_dw_b[layer],
                         conf_ln_g[layer], conf_ln_b[layer], w_conf_out[layer], sc_dw_w[layer],
                         w_sc_out[layer], w_attn_out[layer], w_o[layer], norm2_g[layer], w_up[layer],
                         ffn_dw_w[layer], w_down[layer])
    return rms_norm(x, final_g)
```

```python
import functools

import jax
import jax.numpy as jnp
from jax import lax
from jax.experimental import pallas as pl
from jax.experimental.pallas import tpu as pltpu

F32 = jnp.float32
BF16 = jnp.bfloat16

D_MODEL = 1024
HEAD_DIM = 64
HEADS_PER_GROUP = 4
GROUP_WIDTH = HEADS_PER_GROUP * HEAD_DIM
ATTN_PAIRS = ((128, 1), (512, 4), (2048, 16))
N_GROUPS = len(ATTN_PAIRS)
ATTN_WIDTH = N_GROUPS * GROUP_WIDTH
ATTN_BLOCK = 128
ALIBI_MAX_EXP = 8.0
CONF_WIDTH = 768
CONF_KERNEL = 31
SC_WIDTH = 768
SC_KERNEL = 3
D_FF = 2816
FFN_KERNEL = 3
NORM_EPS = 1e-6
COL_Q, COL_K, COL_V = 0, ATTN_WIDTH, 2 * ATTN_WIDTH
COL_CONF = 3 * ATTN_WIDTH
COL_SC = COL_CONF + 2 * CONF_WIDTH
COL_GATES = COL_SC + 3 * SC_WIDTH
IN_WIDTH = COL_GATES + 3 * D_MODEL

MASK_VALUE = -1e30
SUBLANES = 8
CONF_HALO = 32
SHORT_HALO = SUBLANES
MXU_WIDTH = 256

TM_INPROJ = 512
TM_MIXER = 256
TM_FFN = 512
CONV_ROWS = 16
VMEM_LIMIT = 56 * 1024 * 1024


def _sigmoid(v):
    return 1.0 / (1.0 + jnp.exp(-v))


def _resident(shape):
    return pl.BlockSpec(shape, lambda *_: (0,) * len(shape), pipeline_mode=pl.Buffered(1))


def _inproj_kernel(x_ref, g_ref, w_ref, qkv0_ref, qkv1_ref, qkv2_ref, ca_ref, sc_ref, gt_ref):
    x = x_ref[...]
    ms = jnp.mean(x * x, axis=-1, keepdims=True)
    h = ((x * lax.rsqrt(ms + NORM_EPS)) * g_ref[...]).astype(BF16)

    def proj(col0, width):
        return jnp.dot(h, w_ref[:, col0:col0 + width], preferred_element_type=F32).astype(BF16)

    for g, ref in enumerate((qkv0_ref, qkv1_ref, qkv2_ref)):
        for part, col in enumerate((COL_Q, COL_K, COL_V)):
            ref[:, part * GROUP_WIDTH:(part + 1) * GROUP_WIDTH] = proj(col + g * GROUP_WIDTH, GROUP_WIDTH)
    for j in range(2):
        ca_ref[:, j * CONF_WIDTH:(j + 1) * CONF_WIDTH] = proj(COL_CONF + j * CONF_WIDTH, CONF_WIDTH)
    for j in range(3):
        sc_ref[:, j * SC_WIDTH:(j + 1) * SC_WIDTH] = proj(COL_SC + j * SC_WIDTH, SC_WIDTH)
    for j in range(3):
        gt_ref[:, j * D_MODEL:(j + 1) * D_MODEL] = proj(COL_GATES + j * D_MODEL, D_MODEL)


def _inproj(x, gain, w_in):
    T = x.shape[0]
    tm = TM_INPROJ
    row = lambda width: pl.BlockSpec((tm, width), lambda i: (i, 0))
    widths = (3 * GROUP_WIDTH,) * 3 + (2 * CONF_WIDTH, 3 * SC_WIDTH, 3 * D_MODEL)
    return pl.pallas_call(
        _inproj_kernel,
        grid=(T // tm,),
        in_specs=[row(D_MODEL), _resident((1, D_MODEL)), _resident((D_MODEL, IN_WIDTH))],
        out_specs=[row(w) for w in widths],
        out_shape=[jax.ShapeDtypeStruct((T, w), BF16) for w in widths],
        compiler_params=pltpu.CompilerParams(dimension_semantics=("arbitrary",), vmem_limit_bytes=VMEM_LIMIT),
        name="inproj",
    )(x, gain.reshape(1, D_MODEL), w_in)


def _attn_kernel(slope_ref, c0_ref, p0_ref, c1_ref, p1_ref, c2_ref, p2_ref,
                 o0_ref, l0_ref, o1_ref, l1_ref, o2_ref, l2_ref):
    s = pl.program_id(1)
    qi = lax.broadcasted_iota(jnp.int32, (ATTN_BLOCK, ATTN_BLOCK), 0)
    ki = lax.broadcasted_iota(jnp.int32, (ATTN_BLOCK, ATTN_BLOCK), 1)
    steps_cur = (qi - ki).astype(F32)
    steps_prev = steps_cur + float(ATTN_BLOCK)
    ok_cur = ki <= qi
    ok_prev_any = ki >= qi
    head_of_lane = lax.broadcasted_iota(jnp.int32, (1, GROUP_WIDTH), 1) // HEAD_DIM
    nt = (((1,), (1,)), ((), ()))

    groups = ((c0_ref, p0_ref, o0_ref, l0_ref), (c1_ref, p1_ref, o1_ref, l1_ref), (c2_ref, p2_ref, o2_ref, l2_ref))
    for g, (cur_ref, prev_ref, o_ref, l_ref) in enumerate(groups):
        dilation = ATTN_PAIRS[g][1]
        ok_prev = jnp.logical_and(ok_prev_any, s >= dilation)
        q = cur_ref[:, 0:GROUP_WIDTH]
        k_cur = cur_ref[:, GROUP_WIDTH:2 * GROUP_WIDTH]
        v_cur = cur_ref[:, 2 * GROUP_WIDTH:3 * GROUP_WIDTH]
        k_prev = prev_ref[:, GROUP_WIDTH:2 * GROUP_WIDTH]
        v_prev = prev_ref[:, 2 * GROUP_WIDTH:3 * GROUP_WIDTH]
        o_acc = jnp.zeros((ATTN_BLOCK, GROUP_WIDTH), F32)
        lse_acc = jnp.zeros((ATTN_BLOCK, GROUP_WIDTH), F32)
        for h in range(HEADS_PER_GROUP):
            in_head = head_of_lane == h
            q_h = jnp.where(in_head, q, jnp.zeros_like(q))
            slope = slope_ref[g, h]
            s_cur = lax.dot_general(q_h, k_cur, nt, preferred_element_type=F32) * (HEAD_DIM ** -0.5)
            s_prev = lax.dot_general(q_h, k_prev, nt, preferred_element_type=F32) * (HEAD_DIM ** -0.5)
            s_cur = jnp.where(ok_cur, s_cur - slope * steps_cur, MASK_VALUE)
            s_prev = jnp.where(ok_prev, s_prev - slope * steps_prev, MASK_VALUE)
            m = jnp.maximum(jnp.max(s_cur, axis=-1, keepdims=True), jnp.max(s_prev, axis=-1, keepdims=True))
            p_cur = jnp.exp(s_cur - m)
            p_prev = jnp.exp(s_prev - m)
            l = jnp.sum(p_cur, axis=-1, keepdims=True) + jnp.sum(p_prev, axis=-1, keepdims=True)
            o_h = (jnp.dot(p_cur.astype(BF16), v_cur, preferred_element_type=F32)
                   + jnp.dot(p_prev.astype(BF16), v_prev, preferred_element_type=F32))
            o_acc = jnp.where(in_head, o_h / l, o_acc)
            lse_acc = jnp.where(in_head, m + jnp.log(l), lse_acc)
        o_ref[...] = o_acc
        l_ref[...] = lse_acc


def _attention(qkv, slopes, batch, seq):
    T = batch * seq
    blocks_per_seq = seq // ATTN_BLOCK
    in_specs = [pl.BlockSpec(memory_space=pltpu.SMEM)]
    operands = [slopes]
    out_specs, out_shape = [], []
    for g, (_, d) in enumerate(ATTN_PAIRS):
        tiles_per_seq = blocks_per_seq // d

        def cur_map(b, s, d=d, tiles_per_seq=tiles_per_seq):
            return (b * tiles_per_seq + s // d, 0, s % d)

        def prev_map(b, s, d=d, tiles_per_seq=tiles_per_seq):
            return (b * tiles_per_seq + jnp.maximum(s // d - 1, 0), 0, s % d)

        view = qkv[g].reshape(T // (ATTN_BLOCK * d), ATTN_BLOCK, d * 3 * GROUP_WIDTH)
        in_specs += [pl.BlockSpec((None, ATTN_BLOCK, 3 * GROUP_WIDTH), cur_map),
                     pl.BlockSpec((None, ATTN_BLOCK, 3 * GROUP_WIDTH), prev_map)]
        operands += [view, view]
        for _ in range(2):
            out_specs.append(pl.BlockSpec((None, ATTN_BLOCK, GROUP_WIDTH), cur_map))
            out_shape.append(jax.ShapeDtypeStruct((T // (ATTN_BLOCK * d), ATTN_BLOCK, d * GROUP_WIDTH), F32))
    outs = pl.pallas_call(
        _attn_kernel,
        grid=(batch, blocks_per_seq),
        in_specs=in_specs,
        out_specs=out_specs,
        out_shape=out_shape,
        compiler_params=pltpu.CompilerParams(dimension_semantics=("arbitrary", "arbitrary")),
        name="attention",
    )(*operands)
    return [o.reshape(T, GROUP_WIDTH) for o in outs]


def _mixer_kernel(ca_ref, sc_ref, gt_ref, o0_ref, l0_ref, o1_ref, l1_ref, o2_ref, l2_ref, x_ref,
                  cw_ref, cb_ref, lng_ref, lnb_ref, wconf_ref, sw_ref, wsc_ref, wattn_ref, wo_ref,
                  out_ref, a_ext, cx_ext, conv_ref):
    tm = out_ref.shape[0]

    @pl.when(pl.program_id(1) == 0)
    def _():
        a_ext[0:CONF_HALO, :] = jnp.zeros((CONF_HALO, CONF_WIDTH), F32)
        cx_ext[0:SHORT_HALO, :] = jnp.zeros((SHORT_HALO, SC_WIDTH), F32)

    a = ca_ref[:, 0:CONF_WIDTH].astype(F32) * _sigmoid(ca_ref[:, CONF_WIDTH:2 * CONF_WIDTH].astype(F32))
    a_ext[CONF_HALO:CONF_HALO + tm, :] = a
    first_tap = CONF_HALO - (CONF_KERNEL - 1)
    for r0 in range(0, tm, CONV_ROWS):
        acc = jnp.broadcast_to(cb_ref[...], (CONV_ROWS, CONF_WIDTH))
        for k in range(CONF_KERNEL):
            acc = acc + a_ext[r0 + first_tap + k:r0 + first_tap + k + CONV_ROWS, :] * cw_ref[k:k + 1, :]
        conv_ref[r0:r0 + CONV_ROWS, :] = acc
    a_ext[0:CONF_HALO, :] = a_ext[tm:tm + CONF_HALO, :]
    c = conv_ref[...]
    mu = jnp.mean(c, axis=-1, keepdims=True)
    cc = c - mu
    y = cc * lax.rsqrt(jnp.mean(cc * cc, axis=-1, keepdims=True) + NORM_EPS) * lng_ref[...] + lnb_ref[...]
    z = y * _sigmoid(y)
    conf = jnp.dot(z.astype(BF16), wconf_ref[...], preferred_element_type=F32)

    cx = sc_ref[:, SC_WIDTH:2 * SC_WIDTH].astype(F32) * sc_ref[:, 2 * SC_WIDTH:3 * SC_WIDTH].astype(F32)
    cx_ext[SHORT_HALO:SHORT_HALO + tm, :] = cx
    first_tap = SHORT_HALO - (SC_KERNEL - 1)
    sconv = cx_ext[first_tap:first_tap + tm, :] * sw_ref[0:1, :]
    for k in range(1, SC_KERNEL):
        sconv = sconv + cx_ext[first_tap + k:first_tap + k + tm, :] * sw_ref[k:k + 1, :]
    cx_ext[0:SHORT_HALO, :] = cx_ext[tm:tm + SHORT_HALO, :]
    short = jnp.dot((sc_ref[:, 0:SC_WIDTH].astype(F32) * sconv).astype(BF16), wsc_ref[...],
                    preferred_element_type=F32)

    l0, l1, l2 = l0_ref[...], l1_ref[...], l2_ref[...]
    m = jnp.maximum(jnp.maximum(l0, l1), l2)
    e0, e1, e2 = jnp.exp(l0 - m), jnp.exp(l1 - m), jnp.exp(l2 - m)
    o = (e0 * o0_ref[...] + e1 * o1_ref[...] + e2 * o2_ref[...]) / (e0 + e1 + e2)
    attn = jnp.dot(o.astype(BF16), wattn_ref[...], preferred_element_type=F32)

    mixed = (_sigmoid(gt_ref[:, 0:D_MODEL].astype(F32)) * attn
             + _sigmoid(gt_ref[:, D_MODEL:2 * D_MODEL].astype(F32)) * conf
             + _sigmoid(gt_ref[:, 2 * D_MODEL:3 * D_MODEL].astype(F32)) * short)
    out_ref[...] = x_ref[...] + jnp.dot(mixed.astype(BF16), wo_ref[...], preferred_element_type=F32)


def _mixer(ca, sc, gt, attn_outs, x, cw, cb, lng, lnb, wconf, sw, wsc, wattn, wo, batch, seq):
    T = batch * seq
    tm = TM_MIXER
    tiles = seq // tm
    row = lambda width: pl.BlockSpec((tm, width), lambda b, n: (b * tiles + n, 0))
    in_specs = ([row(2 * CONF_WIDTH), row(3 * SC_WIDTH), row(3 * D_MODEL)] + [row(GROUP_WIDTH)] * 6 + [row(D_MODEL)]
                + [_resident(w.shape) for w in (cw, cb, lng, lnb, wconf, sw, wsc, wattn, wo)])
    return pl.pallas_call(
        _mixer_kernel,
        grid=(batch, tiles),
        in_specs=in_specs,
        out_specs=row(D_MODEL),
        out_shape=jax.ShapeDtypeStruct((T, D_MODEL), F32),
        scratch_shapes=[pltpu.VMEM((CONF_HALO + tm, CONF_WIDTH), F32),
                        pltpu.VMEM((SHORT_HALO + tm, SC_WIDTH), F32),
                        pltpu.VMEM((tm, CONF_WIDTH), F32)],
        compiler_params=pltpu.CompilerParams(dimension_semantics=("arbitrary", "arbitrary"),
                                             vmem_limit_bytes=VMEM_LIMIT),
        name="mixer",
    )(ca, sc, gt, *attn_outs, x, cw, cb, lng, lnb, wconf, sw, wsc, wattn, wo)


def _ffn_kernel(x_ref, g_ref, wup_ref, dw_ref, wdn_ref, fg_ref, out_ref, ext_ref, halo_ref, *, final_norm):
    tm = out_ref.shape[0]
    n_chunks = D_FF // MXU_WIDTH

    @pl.when(pl.program_id(1) == 0)
    def _():
        halo_ref[...] = jnp.zeros(halo_ref.shape, F32)

    x = x_ref[...]
    ms = jnp.mean(x * x, axis=-1, keepdims=True)
    h = ((x * lax.rsqrt(ms + NORM_EPS)) * g_ref[...]).astype(BF16)
    first_tap = SHORT_HALO - (FFN_KERNEL - 1)

    def conv_up(slot, col0):
        up = jnp.dot(h, wup_ref[:, col0:col0 + MXU_WIDTH], preferred_element_type=F32)
        ext_ref[slot % 2, 0:SHORT_HALO, :] = halo_ref[slot]
        ext_ref[slot % 2, SHORT_HALO:SHORT_HALO + tm, :] = up
        halo_ref[slot] = up[tm - SHORT_HALO:tm, :]
        out = ext_ref[slot % 2, first_tap:first_tap + tm, :] * dw_ref[0:1, col0:col0 + MXU_WIDTH]
        for k in range(1, FFN_KERNEL):
            out = out + (ext_ref[slot % 2, first_tap + k:first_tap + k + tm, :]
                         * dw_ref[k:k + 1, col0:col0 + MXU_WIDTH])
        return out

    acc = None
    for c in range(n_chunks):
        gate = conv_up(2 * c, c * MXU_WIDTH)
        val = conv_up(2 * c + 1, D_FF + c * MXU_WIDTH)
        act = (gate * _sigmoid(gate) * val).astype(BF16)
        part = jnp.dot(act, wdn_ref[c * MXU_WIDTH:(c + 1) * MXU_WIDTH, :], preferred_element_type=F32)
        acc = part if acc is None else acc + part
    y = x + acc
    if final_norm:
        ms = jnp.mean(y * y, axis=-1, keepdims=True)
        y = (y * lax.rsqrt(ms + NORM_EPS)) * fg_ref[...]
    out_ref[...] = y


def _ffn(x, gain, wup, dw, wdn, final_gain, batch, seq, final_norm):
    T = batch * seq
    tm = TM_FFN
    tiles = seq // tm
    row = pl.BlockSpec((tm, D_MODEL), lambda b, n: (b * tiles + n, 0))
    return pl.pallas_call(
        functools.partial(_ffn_kernel, final_norm=final_norm),
        grid=(batch, tiles),
        in_specs=[row, _resident((1, D_MODEL)), _resident(wup.shape), _resident(dw.shape), _resident(wdn.shape),
                  _resident((1, D_MODEL))],
        out_specs=row,
        out_shape=jax.ShapeDtypeStruct((T, D_MODEL), F32),
        scratch_shapes=[pltpu.VMEM((2, SHORT_HALO + tm, MXU_WIDTH), F32),
                        pltpu.VMEM((2 * (D_FF // MXU_WIDTH), SHORT_HALO, MXU_WIDTH), F32)],
        compiler_params=pltpu.CompilerParams(dimension_semantics=("arbitrary", "arbitrary"),
                                             vmem_limit_bytes=VMEM_LIMIT),
        name="ffn",
    )(x, gain.reshape(1, D_MODEL), wup, dw, wdn, final_gain.reshape(1, D_MODEL))


def kernel(x, norm1_g, w_in, conf_dw_w, conf_dw_b, conf_ln_g, conf_ln_b, w_conf_out, sc_dw_w, w_sc_out, w_attn_out, w_o, norm2_g, w_up, ffn_dw_w, w_down, final_g):
    batch, seq, d_model = x.shape
    depth = w_in.shape[0]
    assert d_model == D_MODEL and w_in.shape[2] == IN_WIDTH and w_up.shape[2] == 2 * D_FF
    assert seq % (ATTN_BLOCK * ATTN_PAIRS[-1][1]) == 0 and seq % TM_INPROJ == 0 and seq % TM_FFN == 0

    n_heads = N_GROUPS * HEADS_PER_GROUP
    slopes = jnp.exp2(-ALIBI_MAX_EXP * jnp.arange(1, n_heads + 1, dtype=F32) / n_heads)
    slopes = slopes.reshape(N_GROUPS, HEADS_PER_GROUP) * jnp.array([[d] for _, d in ATTN_PAIRS], F32)

    h = x.reshape(batch * seq, d_model)
    for layer in range(depth):
        qkv0, qkv1, qkv2, ca, sc, gt = _inproj(h, norm1_g[layer], w_in[layer].astype(BF16))
        attn_outs = _attention((qkv0, qkv1, qkv2), slopes, batch, seq)
        h = _mixer(ca, sc, gt, attn_outs, h,
                   conf_dw_w[layer], conf_dw_b[layer].reshape(1, CONF_WIDTH),
                   conf_ln_g[layer].reshape(1, CONF_WIDTH), conf_ln_b[layer].reshape(1, CONF_WIDTH),
                   w_conf_out[layer].astype(BF16), sc_dw_w[layer], w_sc_out[layer].astype(BF16),
                   w_attn_out[layer].astype(BF16), w_o[layer].astype(BF16), batch, seq)
        h = _ffn(h, norm2_g[layer], w_up[layer].astype(BF16), ffn_dw_w[layer], w_down[layer].astype(BF16),
                 final_g, batch, seq, final_norm=(layer == depth - 1))
    return h.reshape(batch, seq, d_model)
```

```python
import functools

import jax
import jax.numpy as jnp
from jax import lax
from jax.experimental import pallas as pl
from jax.experimental.pallas import tpu as pltpu

F32 = jnp.float32
BF16 = jnp.bfloat16

D_MODEL = 1024
HEAD_DIM = 64
HEADS_PER_GROUP = 4
GROUP_WIDTH = HEADS_PER_GROUP * HEAD_DIM
QKV_WIDTH = 3 * GROUP_WIDTH
ATTN_PAIRS = ((128, 1), (512, 4), (2048, 16))
N_GROUPS = len(ATTN_PAIRS)
ATTN_WIDTH = N_GROUPS * GROUP_WIDTH
ATTN_BLOCK = 128
ALIBI_MAX_EXP = 8.0
CONF_WIDTH = 768
CONF_KERNEL = 31
SC_WIDTH = 768
SC_KERNEL = 3
D_FF = 2816
FFN_KERNEL = 3
NORM_EPS = 1e-6
COL_Q, COL_K, COL_V = 0, ATTN_WIDTH, 2 * ATTN_WIDTH
COL_CONF = 3 * ATTN_WIDTH
COL_SC = COL_CONF + 2 * CONF_WIDTH
COL_GATES = COL_SC + 3 * SC_WIDTH
IN_WIDTH = COL_GATES + 3 * D_MODEL

MASK_VALUE = -1e30
SUBLANES = 8
LANES = 128
CONF_HALO = 32
SHORT_HALO = SUBLANES
MXU_WIDTH = 256

TM = 512
CONV_BLOCK = 128
CONV_ROWS = 32
VMEM_LIMIT = 60 * 1024 * 1024


def _sigmoid(v):
    return 1.0 / (1.0 + jnp.exp(-v))


def _resident(shape):
    return pl.BlockSpec(shape, lambda *_: (0,) * len(shape), pipeline_mode=pl.Buffered(1))


def _inproj_kernel(x_ref, g_ref, w_ref, cw_ref, cb_ref, lng_ref, lnb_ref, sw_ref,
                   qkv0_ref, qkv1_ref, qkv2_ref, z_ref, sb_ref, gs_ref,
                   hn_ref, hp_ref, a_ext, a_sh, conv_ref, cx_ext):
    tm = x_ref.shape[0]

    @pl.when(pl.program_id(1) == 0)
    def _():
        a_ext[0:CONF_HALO, :] = jnp.zeros((CONF_HALO, CONF_WIDTH), F32)
        cx_ext[0:SHORT_HALO, :] = jnp.zeros((SHORT_HALO, SC_WIDTH), F32)

    x = x_ref[...]
    ms = jnp.mean(x * x, axis=-1, keepdims=True)
    hn = (x * lax.rsqrt(ms + NORM_EPS)) * g_ref[...]
    for c in range(D_MODEL // LANES):
        hn_ref[c] = hn[:, c * LANES:(c + 1) * LANES]
    h = hn.astype(BF16)

    def proj(lhs, col0, width):
        return jnp.dot(lhs, w_ref[:, col0:col0 + width], preferred_element_type=F32)

    a_ext[CONF_HALO:CONF_HALO + tm, :] = (proj(h, COL_CONF, CONF_WIDTH)
                                          * _sigmoid(proj(h, COL_CONF + CONF_WIDTH, CONF_WIDTH)))
    first_tap = CONF_HALO - (CONF_KERNEL - 1)
    phases = [i for i in range(SUBLANES) if (first_tap + i) % SUBLANES]
    for base in range(0, tm, CONV_BLOCK):
        for slot, i in enumerate(phases):
            rows = CONV_BLOCK + SUBLANES * ((CONF_KERNEL - 1 - i) // SUBLANES)
            a_sh[slot, 0:rows, :] = a_ext[base + first_tap + i:base + first_tap + i + rows, :]
        for r0 in range(0, CONV_BLOCK, CONV_ROWS):
            acc = jnp.broadcast_to(cb_ref[...], (CONV_ROWS, CONF_WIDTH))
            for k in range(CONF_KERNEL):
                i, j = k % SUBLANES, k // SUBLANES
                if i in phases:
                    src = a_sh[phases.index(i), r0 + SUBLANES * j:r0 + SUBLANES * j + CONV_ROWS, :]
                else:
                    row = base + r0 + first_tap + k
                    src = a_ext[row:row + CONV_ROWS, :]
                acc = acc + src * cw_ref[k:k + 1, :]
            conv_ref[base + r0:base + r0 + CONV_ROWS, :] = acc
    a_ext[0:CONF_HALO, :] = a_ext[tm:tm + CONF_HALO, :]
    c = conv_ref[...]
    mu = jnp.mean(c, axis=-1, keepdims=True)
    cc = c - mu
    y = cc * lax.rsqrt(jnp.mean(cc * cc, axis=-1, keepdims=True) + NORM_EPS) * lng_ref[...] + lnb_ref[...]
    z_ref[...] = (y * _sigmoid(y)).astype(BF16)

    cx_ext[SHORT_HALO:SHORT_HALO + tm, :] = (proj(h, COL_SC + SC_WIDTH, SC_WIDTH)
                                             * proj(h, COL_SC + 2 * SC_WIDTH, SC_WIDTH))
    first_tap = SHORT_HALO - (SC_KERNEL - 1)
    sconv = cx_ext[first_tap:first_tap + tm, :] * sw_ref[0:1, :]
    for k in range(1, SC_KERNEL):
        sconv = sconv + cx_ext[first_tap + k:first_tap + k + tm, :] * sw_ref[k:k + 1, :]
    cx_ext[0:SHORT_HALO, :] = cx_ext[tm:tm + SHORT_HALO, :]
    sb_ref[...] = (proj(h, COL_SC, SC_WIDTH) * sconv).astype(BF16)

    for j in range(3):
        gs_ref[:, j * D_MODEL:(j + 1) * D_MODEL] = _sigmoid(proj(h, COL_GATES + j * D_MODEL, D_MODEL)).astype(BF16)

    for g, ref in enumerate((qkv0_ref, qkv1_ref, qkv2_ref)):
        d = ATTN_PAIRS[g][1]
        rows = tm // d
        if d == 1:
            lhs = h
        else:
            for r in range(d):
                for c in range(D_MODEL // LANES):
                    hp_ref[r * rows:(r + 1) * rows, c * LANES:(c + 1) * LANES] = (
                        hn_ref[c, pl.ds(r, rows, stride=d), :].astype(BF16))
            lhs = hp_ref[...]
        for part, col in enumerate((COL_Q, COL_K, COL_V)):
            res = proj(lhs, col + g * GROUP_WIDTH, GROUP_WIDTH).astype(BF16)
            for r in range(d):
                c0 = r * QKV_WIDTH + part * GROUP_WIDTH
                piece = res[r * rows:(r + 1) * rows, :]
                if len(ref.shape) == 3:
                    ref[:, :, c0:c0 + GROUP_WIDTH] = piece.reshape(ref.shape[0], ATTN_BLOCK, GROUP_WIDTH)
                else:
                    ref[:, c0:c0 + GROUP_WIDTH] = piece


def _inproj(x, gain, w_in, cw, cb, lng, lnb, sw, batch, seq):
    T = batch * seq
    tm = TM
    tiles = seq // tm
    row = lambda width: pl.BlockSpec((tm, width), lambda b, n: (b * tiles + n, 0))
    out_specs, out_shape = [], []
    for _, d in ATTN_PAIRS:
        per_slab = ATTN_BLOCK * d // tm
        if per_slab <= 1:
            slabs = tm // (ATTN_BLOCK * d)
            spec = pl.BlockSpec((slabs, ATTN_BLOCK, d * QKV_WIDTH), lambda b, n: (b * tiles + n, 0, 0))
        else:
            spec = pl.BlockSpec((None, tm // d, d * QKV_WIDTH),
                                lambda b, n, per_slab=per_slab: ((b * tiles + n) // per_slab, (b * tiles + n) % per_slab, 0))
        out_specs.append(spec)
        out_shape.append(jax.ShapeDtypeStruct((T // (ATTN_BLOCK * d), ATTN_BLOCK, d * QKV_WIDTH), BF16))
    for width in (CONF_WIDTH, SC_WIDTH, 3 * D_MODEL):
        out_specs.append(row(width))
        out_shape.append(jax.ShapeDtypeStruct((T, width), BF16))
    n_phase = SUBLANES - 1
    return pl.pallas_call(
        _inproj_kernel,
        grid=(batch, tiles),
        in_specs=[row(D_MODEL), _resident((1, D_MODEL)), _resident((D_MODEL, IN_WIDTH))]
                 + [_resident(w.shape) for w in (cw, cb, lng, lnb, sw)],
        out_specs=out_specs,
        out_shape=out_shape,
        scratch_shapes=[pltpu.VMEM((D_MODEL // LANES, tm, LANES), F32),
                        pltpu.VMEM((tm, D_MODEL), BF16),
                        pltpu.VMEM((CONF_HALO + tm, CONF_WIDTH), F32),
                        pltpu.VMEM((n_phase, CONV_BLOCK + CONF_HALO - SUBLANES, CONF_WIDTH), F32),
                        pltpu.VMEM((tm, CONF_WIDTH), F32),
                        pltpu.VMEM((SHORT_HALO + tm, SC_WIDTH), F32)],
        compiler_params=pltpu.CompilerParams(dimension_semantics=("arbitrary", "arbitrary"),
                                             vmem_limit_bytes=VMEM_LIMIT),
        name="inproj",
    )(x, gain.reshape(1, D_MODEL), w_in, cw, cb, lng, lnb, sw)


def _attn_kernel(slope_ref, c0_ref, p0_ref, c1_ref, p1_ref, c2_ref, p2_ref,
                 o0_ref, l0_ref, o1_ref, l1_ref, o2_ref, l2_ref):
    s = pl.program_id(1)
    qi = lax.broadcasted_iota(jnp.int32, (ATTN_BLOCK, ATTN_BLOCK), 0)
    ki = lax.broadcasted_iota(jnp.int32, (ATTN_BLOCK, ATTN_BLOCK), 1)
    steps_cur = (qi - ki).astype(F32)
    steps_prev = steps_cur + float(ATTN_BLOCK)
    ok_cur = ki <= qi
    ok_prev_any = ki >= qi
    head_of_lane = lax.broadcasted_iota(jnp.int32, (1, GROUP_WIDTH), 1) // HEAD_DIM
    nt = (((1,), (1,)), ((), ()))

    groups = ((c0_ref, p0_ref, o0_ref, l0_ref), (c1_ref, p1_ref, o1_ref, l1_ref), (c2_ref, p2_ref, o2_ref, l2_ref))
    for g, (cur_ref, prev_ref, o_ref, l_ref) in enumerate(groups):
        dilation = ATTN_PAIRS[g][1]
        ok_prev = jnp.logical_and(ok_prev_any, s >= dilation)
        q = cur_ref[:, 0:GROUP_WIDTH]
        k_cur = cur_ref[:, GROUP_WIDTH:2 * GROUP_WIDTH]
        v_cur = cur_ref[:, 2 * GROUP_WIDTH:3 * GROUP_WIDTH]
        k_prev = prev_ref[:, GROUP_WIDTH:2 * GROUP_WIDTH]
        v_prev = prev_ref[:, 2 * GROUP_WIDTH:3 * GROUP_WIDTH]
        o_acc = jnp.zeros((ATTN_BLOCK, GROUP_WIDTH), F32)
        lse_acc = jnp.zeros((ATTN_BLOCK, GROUP_WIDTH), F32)
        for h in range(HEADS_PER_GROUP):
            in_head = head_of_lane == h
            q_h = jnp.where(in_head, q, jnp.zeros_like(q))
            slope = slope_ref[g, h]
            s_cur = lax.dot_general(q_h, k_cur, nt, preferred_element_type=F32) * (HEAD_DIM ** -0.5)
            s_prev = lax.dot_general(q_h, k_prev, nt, preferred_element_type=F32) * (HEAD_DIM ** -0.5)
            s_cur = jnp.where(ok_cur, s_cur - slope * steps_cur, MASK_VALUE)
            s_prev = jnp.where(ok_prev, s_prev - slope * steps_prev, MASK_VALUE)
            m = jnp.maximum(jnp.max(s_cur, axis=-1, keepdims=True), jnp.max(s_prev, axis=-1, keepdims=True))
            p_cur = jnp.exp(s_cur - m)
            p_prev = jnp.exp(s_prev - m)
            l = jnp.sum(p_cur, axis=-1, keepdims=True) + jnp.sum(p_prev, axis=-1, keepdims=True)
            o_h = (jnp.dot(p_cur.astype(BF16), v_cur, preferred_element_type=F32)
                   + jnp.dot(p_prev.astype(BF16), v_prev, preferred_element_type=F32))
            o_acc = jnp.where(in_head, o_h / l, o_acc)
            lse_acc = jnp.where(in_head, m + jnp.log(l), lse_acc)
        o_ref[...] = o_acc
        l_ref[...] = lse_acc


def _attention(qkv, slopes, batch, seq):
    T = batch * seq
    blocks_per_seq = seq // ATTN_BLOCK
    in_specs = [pl.BlockSpec(memory_space=pltpu.SMEM)]
    operands = [slopes]
    out_specs, out_shape = [], []
    for g, (_, d) in enumerate(ATTN_PAIRS):
        slabs_per_seq = blocks_per_seq // d

        def cur_map(b, s, d=d, slabs_per_seq=slabs_per_seq):
            return (b * slabs_per_seq + s // d, 0, s % d)

        def prev_map(b, s, d=d, slabs_per_seq=slabs_per_seq):
            return (b * slabs_per_seq + jnp.maximum(s // d - 1, 0), 0, s % d)

        in_specs += [pl.BlockSpec((None, ATTN_BLOCK, QKV_WIDTH), cur_map),
                     pl.BlockSpec((None, ATTN_BLOCK, QKV_WIDTH), prev_map)]
        operands += [qkv[g], qkv[g]]
        for _ in range(2):
            out_specs.append(pl.BlockSpec((None, ATTN_BLOCK, GROUP_WIDTH), cur_map))
            out_shape.append(jax.ShapeDtypeStruct((T // (ATTN_BLOCK * d), ATTN_BLOCK, d * GROUP_WIDTH), F32))
    return pl.pallas_call(
        _attn_kernel,
        grid=(batch, blocks_per_seq),
        in_specs=in_specs,
        out_specs=out_specs,
        out_shape=out_shape,
        compiler_params=pltpu.CompilerParams(dimension_semantics=("arbitrary", "arbitrary")),
        name="attention",
    )(*operands)


def _mixer_kernel(z_ref, sb_ref, gs_ref, o0_ref, l0_ref, o1_ref, l1_ref, o2_ref, l2_ref, x_ref,
                  wconf_ref, wsc_ref, wattn_ref, wo_ref, out_ref, nat_ref):
    tm = out_ref.shape[0]

    def natural(ref, slot, d):
        if d == 1:
            return ref[...].reshape(tm, GROUP_WIDTH)
        rows = tm // d
        blk = ref[...].reshape(rows, d * GROUP_WIDTH)
        for r in range(d):
            for c in range(GROUP_WIDTH // LANES):
                c0 = r * GROUP_WIDTH + c * LANES
                nat_ref[slot, c, pl.ds(r, rows, stride=d), :] = blk[:, c0:c0 + LANES]
        return jnp.concatenate([nat_ref[slot, c] for c in range(GROUP_WIDTH // LANES)], axis=1)

    outs, lses = [], []
    for g, (o_ref, l_ref) in enumerate(((o0_ref, l0_ref), (o1_ref, l1_ref), (o2_ref, l2_ref))):
        d = ATTN_PAIRS[g][1]
        outs.append(natural(o_ref, 2 * (g - 1), d))
        lses.append(natural(l_ref, 2 * (g - 1) + 1, d))
    m = jnp.maximum(jnp.maximum(lses[0], lses[1]), lses[2])
    e = [jnp.exp(l - m) for l in lses]
    o = (e[0] * outs[0] + e[1] * outs[1] + e[2] * outs[2]) / (e[0] + e[1] + e[2])
    attn = jnp.dot(o.astype(BF16), wattn_ref[...], preferred_element_type=F32)
    conf = jnp.dot(z_ref[...], wconf_ref[...], preferred_element_type=F32)
    short = jnp.dot(sb_ref[...], wsc_ref[...], preferred_element_type=F32)
    mixed = (gs_ref[:, 0:D_MODEL].astype(F32) * attn
             + gs_ref[:, D_MODEL:2 * D_MODEL].astype(F32) * conf
             + gs_ref[:, 2 * D_MODEL:3 * D_MODEL].astype(F32) * short)
    out_ref[...] = x_ref[...] + jnp.dot(mixed.astype(BF16), wo_ref[...], preferred_element_type=F32)


def _mixer(z, sb, gs, attn_outs, x, wconf, wsc, wattn, wo, batch, seq):
    T = batch * seq
    tm = TM
    tiles = seq // tm
    row = lambda width: pl.BlockSpec((tm, width), lambda b, n: (b * tiles + n, 0))
    attn_specs = []
    for _, d in ATTN_PAIRS:
        per_slab = ATTN_BLOCK * d // tm
        if per_slab <= 1:
            slabs = tm // (ATTN_BLOCK * d)
            spec = pl.BlockSpec((slabs, ATTN_BLOCK, d * GROUP_WIDTH), lambda b, n: (b * tiles + n, 0, 0))
        else:
            spec = pl.BlockSpec((None, tm // d, d * GROUP_WIDTH),
                                lambda b, n, per_slab=per_slab: ((b * tiles + n) // per_slab, (b * tiles + n) % per_slab, 0))
        attn_specs += [spec, spec]
    return pl.pallas_call(
        _mixer_kernel,
        grid=(batch, tiles),
        in_specs=[row(CONF_WIDTH), row(SC_WIDTH), row(3 * D_MODEL)] + attn_specs + [row(D_MODEL)]
                 + [_resident(w.shape) for w in (wconf, wsc, wattn, wo)],
        out_specs=row(D_MODEL),
        out_shape=jax.ShapeDtypeStruct((T, D_MODEL), F32),
        scratch_shapes=[pltpu.VMEM((2 * (N_GROUPS - 1), GROUP_WIDTH // LANES, tm, LANES), F32)],
        compiler_params=pltpu.CompilerParams(dimension_semantics=("arbitrary", "arbitrary"),
                                             vmem_limit_bytes=VMEM_LIMIT),
        name="mixer",
    )(z, sb, gs, *attn_outs, x, wconf, wsc, wattn, wo)


def _ffn_kernel(x_ref, g_ref, wup_ref, dw_ref, wdn_ref, fg_ref, out_ref, up_ref, *, final_norm):
    tm = out_ref.shape[0]
    n_chunks = D_FF // MXU_WIDTH

    @pl.when(pl.program_id(1) == 0)
    def _():
        up_ref[0:SHORT_HALO, :] = jnp.zeros((SHORT_HALO, 2 * D_FF), F32)

    x = x_ref[...]
    ms = jnp.mean(x * x, axis=-1, keepdims=True)
    h = ((x * lax.rsqrt(ms + NORM_EPS)) * g_ref[...]).astype(BF16)
    first_tap = SHORT_HALO - (FFN_KERNEL - 1)

    def conv_up(col0):
        cols = slice(col0, col0 + MXU_WIDTH)
        up_ref[SHORT_HALO:SHORT_HALO + tm, cols] = jnp.dot(h, wup_ref[:, cols], preferred_element_type=F32)
        out = up_ref[first_tap:first_tap + tm, cols] * dw_ref[0:1, cols]
        for k in range(1, FFN_KERNEL):
            out = out + up_ref[first_tap + k:first_tap + k + tm, cols] * dw_ref[k:k + 1, cols]
        return out

    acc = None
    for c in range(n_chunks):
        gate = conv_up(c * MXU_WIDTH)
        val = conv_up(D_FF + c * MXU_WIDTH)
        act = (gate * _sigmoid(gate) * val).astype(BF16)
        part = jnp.dot(act, wdn_ref[c * MXU_WIDTH:(c + 1) * MXU_WIDTH, :], preferred_element_type=F32)
        acc = part if acc is None else acc + part
    up_ref[0:SHORT_HALO, :] = up_ref[tm:tm + SHORT_HALO, :]
    y = x + acc
    if final_norm:
        ms = jnp.mean(y * y, axis=-1, keepdims=True)
        y = (y * lax.rsqrt(ms + NORM_EPS)) * fg_ref[...]
    out_ref[...] = y


def _ffn(x, gain, wup, dw, wdn, final_gain, batch, seq, final_norm):
    T = batch * seq
    tm = TM
    tiles = seq // tm
    row = pl.BlockSpec((tm, D_MODEL), lambda b, n: (b * tiles + n, 0))
    return pl.pallas_call(
        functools.partial(_ffn_kernel, final_norm=final_norm),
        grid=(batch, tiles),
        in_specs=[row, _resident((1, D_MODEL)), _resident(wup.shape), _resident(dw.shape), _resident(wdn.shape),
                  _resident((1, D_MODEL))],
        out_specs=row,
        out_shape=jax.ShapeDtypeStruct((T, D_MODEL), F32),
        scratch_shapes=[pltpu.VMEM((SHORT_HALO + tm, 2 * D_FF), F32)],
        compiler_params=pltpu.CompilerParams(dimension_semantics=("arbitrary", "arbitrary"),
                                             vmem_limit_bytes=VMEM_LIMIT),
        name="ffn",
    )(x, gain.reshape(1, D_MODEL), wup, dw, wdn, final_gain.reshape(1, D_MODEL))


def kernel(x, norm1_g, w_in, conf_dw_w, conf_dw_b, conf_ln_g, conf_ln_b, w_conf_out, sc_dw_w, w_sc_out, w_attn_out, w_o, norm2_g, w_up, ffn_dw_w, w_down, final_g):
    batch, seq, d_model = x.shape
    depth = w_in.shape[0]
    assert d_model == D_MODEL and w_in.shape[2] == IN_WIDTH and w_up.shape[2] == 2 * D_FF
    assert seq % (ATTN_BLOCK * ATTN_PAIRS[-1][1]) == 0 and seq % TM == 0

    n_heads = N_GROUPS * HEADS_PER_GROUP
    slopes = jnp.exp2(-ALIBI_MAX_EXP * jnp.arange(1, n_heads + 1, dtype=F32) / n_heads)
    slopes = slopes.reshape(N_GROUPS, HEADS_PER_GROUP) * jnp.array([[d] for _, d in ATTN_PAIRS], F32)

    h = x.reshape(batch * seq, d_model)
    for layer in range(depth):
        *qkv, z, sb, gs = _inproj(h, norm1_g[layer], w_in[layer].astype(BF16),
                                  conf_dw_w[layer], conf_dw_b[layer].reshape(1, CONF_WIDTH),
                                  conf_ln_g[layer].reshape(1, CONF_WIDTH), conf_ln_b[layer].reshape(1, CONF_WIDTH),
                                  sc_dw_w[layer], batch, seq)
        attn_outs = _attention(qkv, slopes, batch, seq)
        h = _mixer(z, sb, gs, attn_outs, h, w_conf_out[layer].astype(BF16), w_sc_out[layer].astype(BF16),
                   w_attn_out[layer].astype(BF16), w_o[layer].astype(BF16), batch, seq)
        h = _ffn(h, norm2_g[layer], w_up[layer].astype(BF16), ffn_dw_w[layer], w_down[layer].astype(BF16),
                 final_g, batch, seq, final_norm=(layer == depth - 1))
    return h.reshape(batch, seq, d_model)
```

```python
import functools

import jax
import jax.numpy as jnp
from jax import lax
from jax.experimental import pallas as pl
from jax.experimental.pallas import tpu as pltpu

F32 = jnp.float32
BF16 = jnp.bfloat16

D_MODEL = 1024
HEAD_DIM = 64
HEADS_PER_GROUP = 4
GROUP_WIDTH = HEADS_PER_GROUP * HEAD_DIM
QKV_WIDTH = 3 * GROUP_WIDTH
ATTN_PAIRS = ((128, 1), (512, 4), (2048, 16))
N_GROUPS = len(ATTN_PAIRS)
ATTN_WIDTH = N_GROUPS * GROUP_WIDTH
ATTN_BLOCK = 128
ALIBI_MAX_EXP = 8.0
CONF_WIDTH = 768
CONF_KERNEL = 31
SC_WIDTH = 768
SC_KERNEL = 3
D_FF = 2816
FFN_KERNEL = 3
NORM_EPS = 1e-6
COL_Q, COL_K, COL_V = 0, ATTN_WIDTH, 2 * ATTN_WIDTH
COL_CONF = 3 * ATTN_WIDTH
COL_SC = COL_CONF + 2 * CONF_WIDTH
COL_GATES = COL_SC + 3 * SC_WIDTH
IN_WIDTH = COL_GATES + 3 * D_MODEL

MASK_VALUE = -1e30
SUBLANES = 8
LANES = 128
CONF_HALO = 32
SHORT_HALO = SUBLANES
MXU_WIDTH = 256

TM = 512
CONV_BLOCK = 128
CONV_ROWS = 32
VMEM_LIMIT = 60 * 1024 * 1024


def _sigmoid(v):
    return 1.0 / (1.0 + jnp.exp(-v))


def _resident(shape):
    return pl.BlockSpec(shape, lambda *_: (0,) * len(shape), pipeline_mode=pl.Buffered(1))


def _inproj_kernel(x_ref, g_ref, w_ref, cw_ref, cb_ref, lng_ref, lnb_ref, sw_ref,
                   qkv0_ref, qkv1_ref, qkv2_ref, z_ref, sb_ref, gs_ref,
                   hn_ref, hp_ref, a_ext, a_sh, conv_ref, cx_ext):
    tm = x_ref.shape[0]

    @pl.when(pl.program_id(1) == 0)
    def _():
        a_ext[0:CONF_HALO, :] = jnp.zeros((CONF_HALO, CONF_WIDTH), F32)
        cx_ext[0:SHORT_HALO, :] = jnp.zeros((SHORT_HALO, SC_WIDTH), F32)

    x = x_ref[...]
    ms = jnp.mean(x * x, axis=-1, keepdims=True)
    hn = (x * lax.rsqrt(ms + NORM_EPS)) * g_ref[...]
    for c in range(D_MODEL // LANES):
        hn_ref[c] = hn[:, c * LANES:(c + 1) * LANES]
    h = hn.astype(BF16)

    def proj(lhs, col0, width):
        return jnp.dot(lhs, w_ref[:, col0:col0 + width], preferred_element_type=F32)

    a_ext[CONF_HALO:CONF_HALO + tm, :] = (proj(h, COL_CONF, CONF_WIDTH)
                                          * _sigmoid(proj(h, COL_CONF + CONF_WIDTH, CONF_WIDTH)))
    first_tap = CONF_HALO - (CONF_KERNEL - 1)
    phases = [i for i in range(SUBLANES) if (first_tap + i) % SUBLANES]
    for base in range(0, tm, CONV_BLOCK):
        for slot, i in enumerate(phases):
            rows = CONV_BLOCK + SUBLANES * ((CONF_KERNEL - 1 - i) // SUBLANES)
            a_sh[slot, 0:rows, :] = a_ext[base + first_tap + i:base + first_tap + i + rows, :]
        for r0 in range(0, CONV_BLOCK, CONV_ROWS):
            acc = jnp.broadcast_to(cb_ref[...], (CONV_ROWS, CONF_WIDTH))
            for k in range(CONF_KERNEL):
                i, j = k % SUBLANES, k // SUBLANES
                if i in phases:
                    src = a_sh[phases.index(i), r0 + SUBLANES * j:r0 + SUBLANES * j + CONV_ROWS, :]
                else:
                    row = base + r0 + first_tap + k
                    src = a_ext[row:row + CONV_ROWS, :]
                acc = acc + src * cw_ref[k:k + 1, :]
            conv_ref[base + r0:base + r0 + CONV_ROWS, :] = acc
    a_ext[0:CONF_HALO, :] = a_ext[tm:tm + CONF_HALO, :]
    c = conv_ref[...]
    mu = jnp.mean(c, axis=-1, keepdims=True)
    cc = c - mu
    y = cc * lax.rsqrt(jnp.mean(cc * cc, axis=-1, keepdims=True) + NORM_EPS) * lng_ref[...] + lnb_ref[...]
    z_ref[...] = (y * _sigmoid(y)).astype(BF16)

    cx_ext[SHORT_HALO:SHORT_HALO + tm, :] = (proj(h, COL_SC + SC_WIDTH, SC_WIDTH)
                                             * proj(h, COL_SC + 2 * SC_WIDTH, SC_WIDTH))
    first_tap = SHORT_HALO - (SC_KERNEL - 1)
    sconv = cx_ext[first_tap:first_tap + tm, :] * sw_ref[0:1, :]
    for k in range(1, SC_KERNEL):
        sconv = sconv + cx_ext[first_tap + k:first_tap + k + tm, :] * sw_ref[k:k + 1, :]
    cx_ext[0:SHORT_HALO, :] = cx_ext[tm:tm + SHORT_HALO, :]
    sb_ref[...] = (proj(h, COL_SC, SC_WIDTH) * sconv).astype(BF16)

    for j in range(3):
        gs_ref[:, j * D_MODEL:(j + 1) * D_MODEL] = _sigmoid(proj(h, COL_GATES + j * D_MODEL, D_MODEL)).astype(BF16)

    for g, ref in enumerate((qkv0_ref, qkv1_ref, qkv2_ref)):
        d = ATTN_PAIRS[g][1]
        rows = tm // d
        if d == 1:
            lhs = h
        else:
            for r in range(d):
                for c in range(D_MODEL // LANES):
                    hp_ref[r * rows:(r + 1) * rows, c * LANES:(c + 1) * LANES] = (
                        hn_ref[c, pl.ds(r, rows, stride=d), :].astype(BF16))
            lhs = hp_ref[...]
        for part, col in enumerate((COL_Q, COL_K, COL_V)):
            res = proj(lhs, col + g * GROUP_WIDTH, GROUP_WIDTH).astype(BF16)
            for r in range(d):
                c0 = r * QKV_WIDTH + part * GROUP_WIDTH
                piece = res[r * rows:(r + 1) * rows, :]
                if len(ref.shape) == 3:
                    ref[:, :, c0:c0 + GROUP_WIDTH] = piece.reshape(ref.shape[0], ATTN_BLOCK, GROUP_WIDTH)
                else:
                    ref[:, c0:c0 + GROUP_WIDTH] = piece


def _inproj(x, gain, w_in, cw, cb, lng, lnb, sw, batch, seq):
    T = batch * seq
    tm = TM
    tiles = seq // tm
    row = lambda width: pl.BlockSpec((tm, width), lambda b, n: (b * tiles + n, 0))
    out_specs, out_shape = [], []
    for _, d in ATTN_PAIRS:
        per_slab = ATTN_BLOCK * d // tm
        if per_slab <= 1:
            slabs = tm // (ATTN_BLOCK * d)
            spec = pl.BlockSpec((slabs, ATTN_BLOCK, d * QKV_WIDTH), lambda b, n: (b * tiles + n, 0, 0))
        else:
            spec = pl.BlockSpec((None, tm // d, d * QKV_WIDTH),
                                lambda b, n, per_slab=per_slab: ((b * tiles + n) // per_slab, (b * tiles + n) % per_slab, 0))
        out_specs.append(spec)
        out_shape.append(jax.ShapeDtypeStruct((T // (ATTN_BLOCK * d), ATTN_BLOCK, d * QKV_WIDTH), BF16))
    for width in (CONF_WIDTH, SC_WIDTH, 3 * D_MODEL):
        out_specs.append(row(width))
        out_shape.append(jax.ShapeDtypeStruct((T, width), BF16))
    n_phase = SUBLANES - 1
    return pl.pallas_call(
        _inproj_kernel,
        grid=(batch, tiles),
        in_specs=[row(D_MODEL), _resident((1, D_MODEL)), _resident((D_MODEL, IN_WIDTH))]
                 + [_resident(w.shape) for w in (cw, cb, lng, lnb, sw)],
        out_specs=out_specs,
        out_shape=out_shape,
        scratch_shapes=[pltpu.VMEM((D_MODEL // LANES, tm, LANES), F32),
                        pltpu.VMEM((tm, D_MODEL), BF16),
                        pltpu.VMEM((CONF_HALO + tm, CONF_WIDTH), F32),
                        pltpu.VMEM((n_phase, CONV_BLOCK + CONF_HALO - SUBLANES, CONF_WIDTH), F32),
                        pltpu.VMEM((tm, CONF_WIDTH), F32),
                        pltpu.VMEM((SHORT_HALO + tm, SC_WIDTH), F32)],
        compiler_params=pltpu.CompilerParams(dimension_semantics=("arbitrary", "arbitrary"),
                                             vmem_limit_bytes=VMEM_LIMIT),
        name="inproj",
    )(x, gain.reshape(1, D_MODEL), w_in, cw, cb, lng, lnb, sw)


def _attn_kernel(slope_ref, c0_ref, p0_ref, c1_ref, p1_ref, c2_ref, p2_ref,
                 o0_ref, l0_ref, o1_ref, l1_ref, o2_ref, l2_ref, bias_ref):
    s = pl.program_id(1)

    @pl.when(jnp.logical_and(pl.program_id(0) == 0, s == 0))
    def _():
        qi = lax.broadcasted_iota(jnp.int32, (ATTN_BLOCK, 2 * ATTN_BLOCK), 0)
        ki = lax.broadcasted_iota(jnp.int32, (ATTN_BLOCK, 2 * ATTN_BLOCK), 1)
        steps = qi + ATTN_BLOCK - ki
        ok = jnp.logical_and(steps >= 0, steps <= ATTN_BLOCK)
        ok_first = jnp.logical_and(ok, ki >= ATTN_BLOCK)
        steps = steps.astype(F32)
        for g in range(N_GROUPS):
            for h in range(HEADS_PER_GROUP):
                bias = -(slope_ref[g, h] * steps)
                bias_ref[g, 0, h * ATTN_BLOCK:(h + 1) * ATTN_BLOCK, :] = jnp.where(ok, bias, MASK_VALUE)
                bias_ref[g, 1, h * ATTN_BLOCK:(h + 1) * ATTN_BLOCK, :] = jnp.where(ok_first, bias, MASK_VALUE)

    head_of_lane = lax.broadcasted_iota(jnp.int32, (1, GROUP_WIDTH), 1) // HEAD_DIM
    nt = (((1,), (1,)), ((), ()))

    groups = ((c0_ref, p0_ref, o0_ref, l0_ref), (c1_ref, p1_ref, o1_ref, l1_ref), (c2_ref, p2_ref, o2_ref, l2_ref))
    for g, (cur_ref, prev_ref, o_ref, l_ref) in enumerate(groups):
        first = (s < ATTN_PAIRS[g][1]).astype(jnp.int32)
        q = cur_ref[:, 0:GROUP_WIDTH] * (HEAD_DIM ** -0.5)
        q4 = jnp.concatenate([jnp.where(head_of_lane == h, q, jnp.zeros_like(q)) for h in range(HEADS_PER_GROUP)],
                             axis=0)
        k = jnp.concatenate([prev_ref[:, GROUP_WIDTH:2 * GROUP_WIDTH], cur_ref[:, GROUP_WIDTH:2 * GROUP_WIDTH]], axis=0)
        v = jnp.concatenate([prev_ref[:, 2 * GROUP_WIDTH:3 * GROUP_WIDTH], cur_ref[:, 2 * GROUP_WIDTH:3 * GROUP_WIDTH]],
                            axis=0)
        sc = lax.dot_general(q4, k, nt, preferred_element_type=F32) + bias_ref[g, first]
        m = jnp.max(sc, axis=-1, keepdims=True)
        p = jnp.exp(sc - m)
        l = jnp.sum(p, axis=-1, keepdims=True)
        pv = jnp.dot(p.astype(BF16), v, preferred_element_type=F32) / l
        lse = m + jnp.log(l)
        o_acc = jnp.zeros((ATTN_BLOCK, GROUP_WIDTH), F32)
        lse_acc = jnp.zeros((ATTN_BLOCK, GROUP_WIDTH), F32)
        for h in range(HEADS_PER_GROUP):
            in_head = head_of_lane == h
            o_acc = jnp.where(in_head, pv[h * ATTN_BLOCK:(h + 1) * ATTN_BLOCK, :], o_acc)
            lse_acc = jnp.where(in_head, lse[h * ATTN_BLOCK:(h + 1) * ATTN_BLOCK, :], lse_acc)
        o_ref[...] = o_acc
        l_ref[...] = lse_acc


def _attention(qkv, slopes, batch, seq):
    T = batch * seq
    blocks_per_seq = seq // ATTN_BLOCK
    in_specs = [pl.BlockSpec(memory_space=pltpu.SMEM)]
    operands = [slopes]
    out_specs, out_shape = [], []
    for g, (_, d) in enumerate(ATTN_PAIRS):
        slabs_per_seq = blocks_per_seq // d

        def cur_map(b, s, d=d, slabs_per_seq=slabs_per_seq):
            return (b * slabs_per_seq + s // d, 0, s % d)

        def prev_map(b, s, d=d, slabs_per_seq=slabs_per_seq):
            return (b * slabs_per_seq + jnp.maximum(s // d - 1, 0), 0, s % d)

        in_specs += [pl.BlockSpec((None, ATTN_BLOCK, QKV_WIDTH), cur_map),
                     pl.BlockSpec((None, ATTN_BLOCK, QKV_WIDTH), prev_map)]
        operands += [qkv[g], qkv[g]]
        for _ in range(2):
            out_specs.append(pl.BlockSpec((None, ATTN_BLOCK, GROUP_WIDTH), cur_map))
            out_shape.append(jax.ShapeDtypeStruct((T // (ATTN_BLOCK * d), ATTN_BLOCK, d * GROUP_WIDTH), F32))
    return pl.pallas_call(
        _attn_kernel,
        grid=(batch, blocks_per_seq),
        in_specs=in_specs,
        out_specs=out_specs,
        out_shape=out_shape,
        scratch_shapes=[pltpu.VMEM((N_GROUPS, 2, HEADS_PER_GROUP * ATTN_BLOCK, 2 * ATTN_BLOCK), F32)],
        compiler_params=pltpu.CompilerParams(dimension_semantics=("arbitrary", "arbitrary")),
        name="attention",
    )(*operands)


def _mixer_kernel(z_ref, sb_ref, gs_ref, o0_ref, l0_ref, o1_ref, l1_ref, o2_ref, l2_ref, x_ref,
                  wconf_ref, wsc_ref, wattn_ref, wo_ref, out_ref, nat_ref):
    tm = out_ref.shape[0]

    def natural(ref, slot, d):
        if d == 1:
            return ref[...].reshape(tm, GROUP_WIDTH)
        rows = tm // d
        blk = ref[...].reshape(rows, d * GROUP_WIDTH)
        for r in range(d):
            for c in range(GROUP_WIDTH // LANES):
                c0 = r * GROUP_WIDTH + c * LANES
                nat_ref[slot, c, pl.ds(r, rows, stride=d), :] = blk[:, c0:c0 + LANES]
        return jnp.concatenate([nat_ref[slot, c] for c in range(GROUP_WIDTH // LANES)], axis=1)

    outs, lses = [], []
    for g, (o_ref, l_ref) in enumerate(((o0_ref, l0_ref), (o1_ref, l1_ref), (o2_ref, l2_ref))):
        d = ATTN_PAIRS[g][1]
        outs.append(natural(o_ref, 2 * (g - 1), d))
        lses.append(natural(l_ref, 2 * (g - 1) + 1, d))
    m = jnp.maximum(jnp.maximum(lses[0], lses[1]), lses[2])
    e = [jnp.exp(l - m) for l in lses]
    o = (e[0] * outs[0] + e[1] * outs[1] + e[2] * outs[2]) / (e[0] + e[1] + e[2])
    attn = jnp.dot(o.astype(BF16), wattn_ref[...], preferred_element_type=F32)
    conf = jnp.dot(z_ref[...], wconf_ref[...], preferred_element_type=F32)
    short = jnp.dot(sb_ref[...], wsc_ref[...], preferred_element_type=F32)
    mixed = (gs_ref[:, 0:D_MODEL].astype(F32) * attn
             + gs_ref[:, D_MODEL:2 * D_MODEL].astype(F32) * conf
             + gs_ref[:, 2 * D_MODEL:3 * D_MODEL].astype(F32) * short)
    out_ref[...] = x_ref[...] + jnp.dot(mixed.astype(BF16), wo_ref[...], preferred_element_type=F32)


def _mixer(z, sb, gs, attn_outs, x, wconf, wsc, wattn, wo, batch, seq):
    T = batch * seq
    tm = TM
    tiles = seq // tm
    row = lambda width: pl.BlockSpec((tm, width), lambda b, n: (b * tiles + n, 0))
    attn_specs = []
    for _, d in ATTN_PAIRS:
        per_slab = ATTN_BLOCK * d // tm
        if per_slab <= 1:
            slabs = tm // (ATTN_BLOCK * d)
            spec = pl.BlockSpec((slabs, ATTN_BLOCK, d * GROUP_WIDTH), lambda b, n: (b * tiles + n, 0, 0))
        else:
            spec = pl.BlockSpec((None, tm // d, d * GROUP_WIDTH),
                                lambda b, n, per_slab=per_slab: ((b * tiles + n) // per_slab, (b * tiles + n) % per_slab, 0))
        attn_specs += [spec, spec]
    return pl.pallas_call(
        _mixer_kernel,
        grid=(batch, tiles),
        in_specs=[row(CONF_WIDTH), row(SC_WIDTH), row(3 * D_MODEL)] + attn_specs + [row(D_MODEL)]
                 + [_resident(w.shape) for w in (wconf, wsc, wattn, wo)],
        out_specs=row(D_MODEL),
        out_shape=jax.ShapeDtypeStruct((T, D_MODEL), F32),
        scratch_shapes=[pltpu.VMEM((2 * (N_GROUPS - 1), GROUP_WIDTH // LANES, tm, LANES), F32)],
        compiler_params=pltpu.CompilerParams(dimension_semantics=("arbitrary", "arbitrary"),
                                             vmem_limit_bytes=VMEM_LIMIT),
        name="mixer",
    )(z, sb, gs, *attn_outs, x, wconf, wsc, wattn, wo)


def _ffn_kernel(x_ref, g_ref, wup_ref, dw_ref, wdn_ref, fg_ref, out_ref, h_ref, up_ref, act_ref, *, final_norm):
    tm = out_ref.shape[0]

    @pl.when(pl.program_id(1) == 0)
    def _():
        up_ref[0:SHORT_HALO, :] = jnp.zeros((SHORT_HALO, 2 * D_FF), F32)

    x = x_ref[...]
    ms = jnp.mean(x * x, axis=-1, keepdims=True)
    h_ref[...] = ((x * lax.rsqrt(ms + NORM_EPS)) * g_ref[...]).astype(BF16)
    first_tap = SHORT_HALO - (FFN_KERNEL - 1)

    def conv_up(col0):
        cols = slice(col0, col0 + MXU_WIDTH)
        up_ref[SHORT_HALO:SHORT_HALO + tm, cols] = jnp.dot(h_ref[...], wup_ref[:, cols], preferred_element_type=F32)
        out = up_ref[first_tap:first_tap + tm, cols] * dw_ref[0:1, cols]
        for k in range(1, FFN_KERNEL):
            out = out + up_ref[first_tap + k:first_tap + k + tm, cols] * dw_ref[k:k + 1, cols]
        return out

    for c in range(D_FF // MXU_WIDTH):
        gate = conv_up(c * MXU_WIDTH)
        val = conv_up(D_FF + c * MXU_WIDTH)
        act_ref[:, c * MXU_WIDTH:(c + 1) * MXU_WIDTH] = (gate * _sigmoid(gate) * val).astype(BF16)
    up_ref[0:SHORT_HALO, :] = up_ref[tm:tm + SHORT_HALO, :]
    y = x + jnp.dot(act_ref[...], wdn_ref[...], preferred_element_type=F32)
    if final_norm:
        ms = jnp.mean(y * y, axis=-1, keepdims=True)
        y = (y * lax.rsqrt(ms + NORM_EPS)) * fg_ref[...]
    out_ref[...] = y


def _ffn(x, gain, wup, dw, wdn, final_gain, batch, seq, final_norm):
    T = batch * seq
    tm = TM
    tiles = seq // tm
    row = pl.BlockSpec((tm, D_MODEL), lambda b, n: (b * tiles + n, 0))
    return pl.pallas_call(
        functools.partial(_ffn_kernel, final_norm=final_norm),
        grid=(batch, tiles),
        in_specs=[row, _resident((1, D_MODEL)), _resident(wup.shape), _resident(dw.shape), _resident(wdn.shape),
                  _resident((1, D_MODEL))],
        out_specs=row,
        out_shape=jax.ShapeDtypeStruct((T, D_MODEL), F32),
        scratch_shapes=[pltpu.VMEM((tm, D_MODEL), BF16),
                        pltpu.VMEM((SHORT_HALO + tm, 2 * D_FF), F32),
                        pltpu.VMEM((tm, D_FF), BF16)],
        compiler_params=pltpu.CompilerParams(dimension_semantics=("arbitrary", "arbitrary"),
                                             vmem_limit_bytes=VMEM_LIMIT),
        name="ffn",
    )(x, gain.reshape(1, D_MODEL), wup, dw, wdn, final_gain.reshape(1, D_MODEL))


def kernel(x, norm1_g, w_in, conf_dw_w, conf_dw_b, conf_ln_g, conf_ln_b, w_conf_out, sc_dw_w, w_sc_out, w_attn_out, w_o, norm2_g, w_up, ffn_dw_w, w_down, final_g):
    batch, seq, d_model = x.shape
    depth = w_in.shape[0]
    assert d_model == D_MODEL and w_in.shape[2] == IN_WIDTH and w_up.shape[2] == 2 * D_FF
    assert seq % (ATTN_BLOCK * ATTN_PAIRS[-1][1]) == 0 and seq % TM == 0

    n_heads = N_GROUPS * HEADS_PER_GROUP
    slopes = jnp.exp2(-ALIBI_MAX_EXP * jnp.arange(1, n_heads + 1, dtype=F32) / n_heads)
    slopes = slopes.reshape(N_GROUPS, HEADS_PER_GROUP) * jnp.array([[d] for _, d in ATTN_PAIRS], F32)

    h = x.reshape(batch * seq, d_model)
    for layer in range(depth):
        *qkv, z, sb, gs = _inproj(h, norm1_g[layer], w_in[layer].astype(BF16),
                                  conf_dw_w[layer], conf_dw_b[layer].reshape(1, CONF_WIDTH),
                                  conf_ln_g[layer].reshape(1, CONF_WIDTH), conf_ln_b[layer].reshape(1, CONF_WIDTH),
                                  sc_dw_w[layer], batch, seq)
        attn_outs = _attention(qkv, slopes, batch, seq)
        h = _mixer(z, sb, gs, attn_outs, h, w_conf_out[layer].astype(BF16), w_sc_out[layer].astype(BF16),
                   w_attn_out[layer].astype(BF16), w_o[layer].astype(BF16), batch, seq)
        h = _ffn(h, norm2_g[layer], w_up[layer].astype(BF16), ffn_dw_w[layer], w_down[layer].astype(BF16),
                 final_g, batch, seq, final_norm=(layer == depth - 1))
    return h.reshape(batch, seq, d_model)
```
